```python
import jax, jax.numpy as jnp
from jax import lax
import numpy as np

D_MODEL = 1024
BATCH = 8
SEQ = 2048
DEPTH = 2
DEC_BATCH = 128
DEC_SEQ = 4
PAST_LEN = 16384
PAGE_SIZE = 128

N_MIXERS = 2
N_A_LAYERS = (DEPTH + 1) // 2
N_B_LAYERS = DEPTH // 2
EXPAND = 2
D_BRANCH = EXPAND * D_MODEL
CHUNK = 128
N_GROUPS = 8
GROUP_DIM = D_BRANCH // N_GROUPS
CONV_W = 31
EPS = 1e-6

kernel_name = "hybrid_chunkmlp_conformer_step"


def rmsnorm(x, g):
    xf = x.astype(jnp.float32)
    y = xf * lax.rsqrt(jnp.mean(xf * xf, axis=-1, keepdims=True) + EPS)
    return (y * g.astype(jnp.float32)).astype(x.dtype)


def layernorm(x, g, b):
    xf = x.astype(jnp.float32)
    mu = jnp.mean(xf, axis=-1, keepdims=True)
    xc = xf - mu
    var = jnp.mean(xc * xc, axis=-1, keepdims=True)
    y = xc * lax.rsqrt(var + EPS)
    return (y * g.astype(jnp.float32) + b.astype(jnp.float32)).astype(x.dtype)


def spatial_gate(v_chunks, w_s, b_s):
    L = v_chunks.shape[2]
    mask = jnp.tril(jnp.ones((L, L), dtype=w_s.dtype))
    w = w_s[:, :L, :L] * mask
    bias = b_s[:, :L].T
    mixed = jnp.einsum('gts,bcsgd->bctgd', w, v_chunks)
    return mixed + bias[None, None, :, :, None]


def chunk_mlp_branch(h, w_in, ln_g, ln_b, w_s, b_s, w_out):
    B, T, _ = h.shape
    proj = jnp.einsum('btd,de->bte', h, w_in)
    u, v, z = jnp.split(proj, 3, axis=-1)
    u = jax.nn.gelu(u, approximate=False)
    v = layernorm(jax.nn.gelu(v, approximate=False), ln_g, ln_b)
    L = min(T, CHUNK)
    v_chunks = v.reshape(B, T // L, L, N_GROUPS, GROUP_DIM)
    mixed = spatial_gate(v_chunks, w_s, b_s).reshape(B, T, D_BRANCH)
    out = u * mixed * jax.nn.silu(z)
    return jnp.einsum('bte,ed->btd', out, w_out), v


def conv_branch(h, left, w_in, conv_w, conv_b, ln_g, ln_b, w_out):
    proj = jnp.einsum('btd,de->bte', h, w_in)
    a, gl, z = jnp.split(proj, 3, axis=-1)
    g = a * jax.nn.sigmoid(gl)
    xp = jnp.concatenate([left.astype(g.dtype), g], axis=1)
    c = lax.conv_general_dilated(
        xp, conv_w[:, None, :].astype(xp.dtype), window_strides=(1,), padding='VALID',
        dimension_numbers=('NWC', 'WIO', 'NWC'), feature_group_count=D_BRANCH)
    c = c + conv_b
    c = jax.nn.silu(layernorm(c, ln_g, ln_b))
    out = c * jax.nn.silu(z)
    y = jnp.einsum('bte,ed->btd', out, w_out)
    return y, xp[:, -(CONV_W - 1):]


def setup_inputs(seed: int = 0) -> dict:
    key = jax.random.key(seed)
    ks = jax.random.split(key, 20)
    f32 = jnp.float32
    D, E = D_MODEL, D_BRANCH
    nrm = lambda k, s, sc: (jax.random.normal(k, s, f32) * sc)
    return {
        "x_prompt": nrm(ks[0], (BATCH, SEQ, D), 1.0),
        "x_sample": nrm(ks[1], (DEC_BATCH, DEC_SEQ, D), 1.0),
        "state_conv": nrm(ks[2], (N_B_LAYERS, DEC_BATCH, CONV_W - 1, E), 0.5),
        "pre_norm_g": 1.0 + nrm(ks[3], (DEPTH, D), 0.05),
        "post_norm_g": 1.0 + nrm(ks[4], (DEPTH, D), 0.05),
        "a_w_in": nrm(ks[5], (N_A_LAYERS, D, 3 * E), D ** -0.5),
        "a_ln_g": 1.0 + nrm(ks[6], (N_A_LAYERS, E), 0.05),
        "a_ln_b": nrm(ks[7], (N_A_LAYERS, E), 0.02),
        "a_w_s": nrm(ks[8], (N_A_LAYERS, N_GROUPS, CHUNK, CHUNK), CHUNK ** -0.5),
        "a_b_s": 1.0 + nrm(ks[9], (N_A_LAYERS, N_GROUPS, CHUNK), 0.1),
        "a_w_out": nrm(ks[10], (N_A_LAYERS, E, D), E ** -0.5),
        "b_w_in": nrm(ks[11], (N_B_LAYERS, D, 3 * E), D ** -0.5),
        "b_conv_w": nrm(ks[12], (N_B_LAYERS, CONV_W, E), CONV_W ** -0.5),
        "b_conv_b": nrm(ks[13], (N_B_LAYERS, E), 0.02),
        "b_ln_g": 1.0 + nrm(ks[14], (N_B_LAYERS, E), 0.05),
        "b_ln_b": nrm(ks[15], (N_B_LAYERS, E), 0.02),
        "b_w_out": nrm(ks[16], (N_B_LAYERS, E, D), E ** -0.5),
    }


def reference(x_prompt, x_sample, state_conv, pre_norm_g, post_norm_g,
              a_w_in, a_ln_g, a_ln_b, a_w_s, a_b_s, a_w_out,
              b_w_in, b_conv_w, b_conv_b, b_ln_g, b_ln_b, b_w_out):
    xp, xs = x_prompt, x_sample
    conv_prompt_list, conv_sample_list, v_sample_list = [], [], []
    for i in range(DEPTH):
        j = i // N_MIXERS
        hp = rmsnorm(xp, pre_norm_g[i])
        hs = rmsnorm(xs, pre_norm_g[i])
        if i % N_MIXERS == 0:
            yp, _ = chunk_mlp_branch(hp, a_w_in[j], a_ln_g[j], a_ln_b[j], a_w_s[j], a_b_s[j], a_w_out[j])
            ys, vs = chunk_mlp_branch(hs, a_w_in[j], a_ln_g[j], a_ln_b[j], a_w_s[j], a_b_s[j], a_w_out[j])
            v_sample_list.append(vs)
        else:
            left_p = jnp.zeros((xp.shape[0], CONV_W - 1, D_BRANCH), dtype=xp.dtype)
            yp, cp = conv_branch(hp, left_p, b_w_in[j], b_conv_w[j], b_conv_b[j], b_ln_g[j], b_ln_b[j], b_w_out[j])
            ys, cs = conv_branch(hs, state_conv[j], b_w_in[j], b_conv_w[j], b_conv_b[j], b_ln_g[j], b_ln_b[j], b_w_out[j])
            conv_prompt_list.append(cp)
            conv_sample_list.append(cs)
        xp = xp + rmsnorm(yp, post_norm_g[i])
        xs = xs + rmsnorm(ys, post_norm_g[i])
    conv_prompt_new = jnp.stack(conv_prompt_list, axis=0)
    conv_sample_new = jnp.stack(conv_sample_list, axis=0)
    chunk_v_sample = jnp.stack(v_sample_list, axis=0)
    return (xp, xs, conv_prompt_new, conv_sample_new, chunk_v_sample)
```

```python
import functools

import jax
import jax.numpy as jnp
import numpy as np
from jax import lax
from jax.experimental import pallas as pl
from jax.experimental.pallas import tpu as pltpu

D_MODEL = 1024
D_BRANCH = 2048
CHUNK = 128
N_GROUPS = 8
GROUP_DIM = D_BRANCH // N_GROUPS
CONV_W = 31
EPS = 1e-6

HIST = 32
ROW_BLOCK = 32
CONV_LANE_BLOCK = 256
SQRT_HALF = float(np.sqrt(0.5).astype(np.float32))
VMEM_LIMIT_BYTES = 56 * 1024 * 1024

F32 = jnp.float32
BF16 = jnp.bfloat16


def _rms(x, g):
    ms = jnp.mean(x * x, axis=-1, keepdims=True)
    return x * lax.rsqrt(ms + EPS) * g


def _ln(x, g, b):
    mu = jnp.mean(x, axis=-1, keepdims=True)
    xc = x - mu
    var = jnp.mean(xc * xc, axis=-1, keepdims=True)
    return xc * lax.rsqrt(var + EPS) * g + b


def _gelu(x):
    return 0.5 * x * (1.0 + lax.erf(x * SQRT_HALF))


def _silu(x):
    return x * jax.nn.sigmoid(x)


def _dot(a, b):
    return jnp.dot(a, b, preferred_element_type=F32)


def _const_spec(shape):
    zeros = (0,) * len(shape)
    return pl.BlockSpec(shape, lambda i: zeros, pipeline_mode=pl.Buffered(1))


def _for_row_blocks(n_rows, body):
    def step(j, carry):
        body(pl.ds(pl.multiple_of(j * ROW_BLOCK, ROW_BLOCK), ROW_BLOCK))
        return carry
    lax.fori_loop(0, n_rows // ROW_BLOCK, step, 0)


def _project(x, pre_g_ref, w_in_ref, hb_ref, p0_ref, p1_ref, p2_ref):
    e = D_BRANCH
    hb_ref[...] = _rms(x, pre_g_ref[...]).astype(BF16)
    p0_ref[...] = _dot(hb_ref[...], w_in_ref[:, 0:e])
    p1_ref[...] = _dot(hb_ref[...], w_in_ref[:, e:2 * e])
    p2_ref[...] = _dot(hb_ref[...], w_in_ref[:, 2 * e:3 * e])


def _mixer_a_kernel(x_ref, pre_g_ref, post_g_ref, w_in_ref, ln_g_ref, ln_b_ref, ws_ref, bias_ref,
                    w_out_ref, *refs, tm, emit_v):
    if emit_v:
        y_ref, v_ref, hb_ref, u_ref, vp_ref, z_ref, vb_ref, o_ref = refs
    else:
        y_ref, hb_ref, u_ref, vp_ref, z_ref, vb_ref, o_ref = refs
    x = x_ref[...]
    _project(x, pre_g_ref, w_in_ref, hb_ref, u_ref, vp_ref, z_ref)

    def gate(rows):
        v = _ln(_gelu(vp_ref[rows, :]), ln_g_ref[...], ln_b_ref[...])
        if emit_v:
            v_ref[rows, :] = v
        vb_ref[rows, :] = v.astype(BF16)
        u_ref[rows, :] = _gelu(u_ref[rows, :]) * _silu(z_ref[rows, :])
    _for_row_blocks(tm, gate)

    for c in range(tm // CHUNK):
        rows = slice(c * CHUNK, (c + 1) * CHUNK)
        for g in range(N_GROUPS):
            lanes = slice(g * GROUP_DIM, (g + 1) * GROUP_DIM)
            mixed = _dot(ws_ref[g], vb_ref[rows, lanes]) + bias_ref[:, lanes]
            o_ref[rows, lanes] = (u_ref[rows, lanes] * mixed).astype(BF16)
    y = _dot(o_ref[...], w_out_ref[...])
    y_ref[...] = x + _rms(y, post_g_ref[...])


def _mixer_a(x2d, pre_g, post_g, w_in, ln_g, ln_b, ws, bias, w_out, *, tm, emit_v):
    n, d = x2d.shape
    e = D_BRANCH
    assert n % tm == 0 and tm % CHUNK == 0
    row_spec = lambda width: pl.BlockSpec((tm, width), lambda i: (i, 0))
    out_shape = [jax.ShapeDtypeStruct((n, d), F32)]
    out_specs = [row_spec(d)]
    if emit_v:
        out_shape.append(jax.ShapeDtypeStruct((n, e), F32))
        out_specs.append(row_spec(e))
    return pl.pallas_call(
        functools.partial(_mixer_a_kernel, tm=tm, emit_v=emit_v),
        grid=(n // tm,),
        in_specs=[
            row_spec(d), _const_spec((1, d)), _const_spec((1, d)), _const_spec((d, 3 * e)),
            _const_spec((1, e)), _const_spec((1, e)), _const_spec((N_GROUPS, CHUNK, CHUNK)),
            _const_spec((CHUNK, e)), _const_spec((e, d)),
        ],
        out_specs=out_specs,
        out_shape=out_shape,
        scratch_shapes=[pltpu.VMEM((tm, d), BF16), pltpu.VMEM((tm, e), F32), pltpu.VMEM((tm, e), F32),
                        pltpu.VMEM((tm, e), F32), pltpu.VMEM((tm, e), BF16), pltpu.VMEM((tm, e), BF16)],
        compiler_params=pltpu.CompilerParams(
            dimension_semantics=("arbitrary",), vmem_limit_bytes=VMEM_LIMIT_BYTES),
        name="mixer_a_sample" if emit_v else "mixer_a_prompt",
    )(x2d, pre_g, post_g, w_in, ln_g, ln_b, ws, bias, w_out)


def _conv_tail(x, c_ref, z_ref, o_ref, ln_g_ref, ln_b_ref, w_out_ref, post_g_ref):
    def act(rows):
        c = _silu(_ln(c_ref[rows, :], ln_g_ref[...], ln_b_ref[...]))
        o_ref[rows, :] = (c * _silu(z_ref[rows, :])).astype(BF16)
    _for_row_blocks(c_ref.shape[0], act)
    y = _dot(o_ref[...], w_out_ref[...])
    return x + _rms(y, post_g_ref[...])


def _mixer_b_prompt_kernel(x_ref, pre_g_ref, post_g_ref, w_in_ref, cw_ref, cb_ref, ln_g_ref,
                           ln_b_ref, w_out_ref, y_ref, tail_ref, hb_ref, a_ref, gl_ref, z_ref,
                           xp_ref, o_ref, *, tm, tiles_per_seq):
    e = D_BRANCH
    i = pl.program_id(0)

    @pl.when(i % tiles_per_seq == 0)
    def _():
        xp_ref[0:HIST, :] = jnp.zeros((HIST, e), F32)

    x = x_ref[...]
    _project(x, pre_g_ref, w_in_ref, hb_ref, a_ref, gl_ref, z_ref)

    def glu(rows):
        xp_ref[pl.ds(rows.start + HIST, ROW_BLOCK), :] = a_ref[rows, :] * jax.nn.sigmoid(gl_ref[rows, :])
    _for_row_blocks(tm, glu)

    base = HIST - (CONV_W - 1)

    for r0 in range(0, tm, ROW_BLOCK):
        for c0 in range(0, e, CONV_LANE_BLOCK):
            lanes = slice(c0, c0 + CONV_LANE_BLOCK)
            acc = None
            for k in range(CONV_W):
                start = r0 + base + k
                term = xp_ref[start:start + ROW_BLOCK, lanes] * cw_ref[k:k + 1, lanes]
                acc = term if acc is None else acc + term
            a_ref[r0:r0 + ROW_BLOCK, lanes] = acc + cb_ref[:, lanes]

    y_ref[...] = _conv_tail(x, a_ref, z_ref, o_ref, ln_g_ref, ln_b_ref, w_out_ref, post_g_ref)
    hist = xp_ref[tm:tm + HIST, :]
    tail_ref[0] = hist
    xp_ref[0:HIST, :] = hist


def _mixer_b_prompt(x2d, pre_g, post_g, w_in, cw, cb, ln_g, ln_b, w_out, *, tm, seq):
    n, d = x2d.shape
    e = D_BRANCH
    assert seq % tm == 0 and n % seq == 0
    tiles_per_seq = seq // tm
    row_spec = pl.BlockSpec((tm, d), lambda i: (i, 0))
    return pl.pallas_call(
        functools.partial(_mixer_b_prompt_kernel, tm=tm, tiles_per_seq=tiles_per_seq),
        grid=(n // tm,),
        in_specs=[
            row_spec, _const_spec((1, d)), _const_spec((1, d)), _const_spec((d, 3 * e)),
            _const_spec((CONV_W, e)), _const_spec((1, e)), _const_spec((1, e)), _const_spec((1, e)),
            _const_spec((e, d)),
        ],
        out_specs=[row_spec, pl.BlockSpec((1, HIST, e), lambda i: (i // tiles_per_seq, 0, 0))],
        out_shape=[jax.ShapeDtypeStruct((n, d), F32),
                   jax.ShapeDtypeStruct((n // seq, HIST, e), F32)],
        scratch_shapes=[pltpu.VMEM((tm, d), BF16), pltpu.VMEM((tm, e), F32), pltpu.VMEM((tm, e), F32),
                        pltpu.VMEM((tm, e), F32), pltpu.VMEM((HIST + tm, e), F32),
                        pltpu.VMEM((tm, e), BF16)],
        compiler_params=pltpu.CompilerParams(
            dimension_semantics=("arbitrary",), vmem_limit_bytes=VMEM_LIMIT_BYTES),
        name="mixer_b_prompt",
    )(x2d, pre_g, post_g, w_in, cw, cb, ln_g, ln_b, w_out)


def _mixer_b_sample_kernel(x_ref, st_ref, pre_g_ref, post_g_ref, w_in_ref, cw_ref, wt_ref, cb_ref,
                           ln_g_ref, ln_b_ref, w_out_ref, y_ref, g_ref, hb_ref, a_ref, gl_ref, z_ref,
                           o_ref, *, bs, t_new):
    d, e = D_MODEL, D_BRANCH
    rows = bs * t_new
    x = x_ref[...].reshape(rows, d)
    _project(x, pre_g_ref, w_in_ref, hb_ref, a_ref, gl_ref, z_ref)
    g = a_ref[...] * jax.nn.sigmoid(gl_ref[...])
    g_ref[...] = g.reshape(t_new, bs, e)
    st = st_ref[...]
    for t in range(t_new):
        c_t = jnp.sum(st * wt_ref[t][None, :, :], axis=1)
        for s in range(t + 1):
            k = CONV_W - 1 - (t - s)
            c_t = c_t + g[s * bs:(s + 1) * bs, :] * cw_ref[k:k + 1, :]
        a_ref[t * bs:(t + 1) * bs, :] = c_t + cb_ref[...]
    y = _conv_tail(x, a_ref, z_ref, o_ref, ln_g_ref, ln_b_ref, w_out_ref, post_g_ref)
    y_ref[...] = y.reshape(t_new, bs, d)


def _mixer_b_sample(x3d, state, pre_g, post_g, w_in, cw, wt, cb, ln_g, ln_b, w_out, *, bs):
    t_new, nb, d = x3d.shape
    e = D_BRANCH
    hist = state.shape[1]
    rows = bs * t_new
    assert nb % bs == 0 and rows % ROW_BLOCK == 0
    return pl.pallas_call(
        functools.partial(_mixer_b_sample_kernel, bs=bs, t_new=t_new),
        grid=(nb // bs,),
        in_specs=[
            pl.BlockSpec((t_new, bs, d), lambda i: (0, i, 0)),
            pl.BlockSpec((bs, hist, e), lambda i: (i, 0, 0)),
            _const_spec((1, d)), _const_spec((1, d)), _const_spec((d, 3 * e)),
            _const_spec((CONV_W, e)), _const_spec((t_new, hist, e)), _const_spec((1, e)),
            _const_spec((1, e)), _const_spec((1, e)), _const_spec((e, d)),
        ],
        out_specs=[pl.BlockSpec((t_new, bs, d), lambda i: (0, i, 0)),
                   pl.BlockSpec((t_new, bs, e), lambda i: (0, i, 0))],
        out_shape=[jax.ShapeDtypeStruct((t_new, nb, d), F32), jax.ShapeDtypeStruct((t_new, nb, e), F32)],
        scratch_shapes=[pltpu.VMEM((rows, d), BF16), pltpu.VMEM((rows, e), F32), pltpu.VMEM((rows, e), F32),
                        pltpu.VMEM((rows, e), F32), pltpu.VMEM((rows, e), BF16)],
        compiler_params=pltpu.CompilerParams(
            dimension_semantics=("arbitrary",), vmem_limit_bytes=VMEM_LIMIT_BYTES),
        name="mixer_b_sample",
    )(x3d, state, pre_g, post_g, w_in, cw, wt, cb, ln_g, ln_b, w_out)


def kernel(x_prompt, x_sample, state_conv, pre_norm_g, post_norm_g, a_w_in, a_ln_g, a_ln_b, a_w_s,
           a_b_s, a_w_out, b_w_in, b_conv_w, b_conv_b, b_ln_g, b_ln_b, b_w_out):
    batch, seq, d = x_prompt.shape
    dec_batch, dec_seq, _ = x_sample.shape
    e = D_BRANCH
    hist = CONV_W - 1
    row = lambda p: p.reshape(1, -1)

    xp = x_prompt.reshape(batch * seq, d)
    xs = x_sample.reshape(dec_batch * dec_seq, d)

    w_in = a_w_in[0].astype(BF16)
    w_out = a_w_out[0].astype(BF16)
    ws_prompt = (a_w_s[0] * jnp.tril(jnp.ones((CHUNK, CHUNK), F32))).astype(BF16)
    bias_prompt = jnp.repeat(a_b_s[0].T, GROUP_DIM, axis=1)
    reps = CHUNK // dec_seq
    w_small = a_w_s[0][:, :dec_seq, :dec_seq] * jnp.tril(jnp.ones((dec_seq, dec_seq), F32))
    ws_sample = jnp.einsum("ab,gts->gatbs", jnp.eye(reps, dtype=F32), w_small)
    ws_sample = ws_sample.reshape(N_GROUPS, CHUNK, CHUNK).astype(BF16)
    bias_sample = jnp.repeat(jnp.tile(a_b_s[0][:, :dec_seq], (1, reps)).T, GROUP_DIM, axis=1)

    a_common = (row(pre_norm_g[0]), row(post_norm_g[0]), w_in, row(a_ln_g[0]), row(a_ln_b[0]))
    (xp,) = _mixer_a(xp, *a_common, ws_prompt, bias_prompt, w_out, tm=256, emit_v=False)
    xs, v_sample = _mixer_a(xs, *a_common, ws_sample, bias_sample, w_out, tm=256, emit_v=True)

    w_in = b_w_in[0].astype(BF16)
    w_out = b_w_out[0].astype(BF16)
    cw = b_conv_w[0]
    b_common = (row(pre_norm_g[1]), row(post_norm_g[1]), w_in)
    b_tail = (row(b_conv_b[0]), row(b_ln_g[0]), row(b_ln_b[0]), w_out)
    xp, tail = _mixer_b_prompt(xp, *b_common, cw, *b_tail, tm=256, seq=seq)
    wt = jnp.stack([jnp.pad(cw[:hist - t], ((t, 0), (0, 0))) for t in range(dec_seq)])
    xs_t = xs.reshape(dec_batch, dec_seq, d).transpose(1, 0, 2)
    xs_t, g_t = _mixer_b_sample(xs_t, state_conv[0], *b_common, cw, wt, *b_tail, bs=32)

    conv_prompt_new = tail[:, HIST - hist:, :][None]
    conv_sample_new = jnp.concatenate([state_conv[0][:, dec_seq:, :], g_t.transpose(1, 0, 2)], axis=1)[None]
    return (xp.reshape(batch, seq, d), xs_t.transpose(1, 0, 2), conv_prompt_new, conv_sample_new,
            v_sample.reshape(dec_batch, dec_seq, e)[None])
```

```python
import functools

import jax
import jax.numpy as jnp
import numpy as np
from jax import lax
from jax.experimental import pallas as pl
from jax.experimental.pallas import tpu as pltpu

D_MODEL = 1024
D_BRANCH = 2048
CHUNK = 128
N_GROUPS = 8
GROUP_DIM = D_BRANCH // N_GROUPS
CONV_W = 31
EPS = 1e-6

HIST = 32
ROW_BLOCK = 32
ELEM_ROWS = 16
LANES = 128
N_SLABS = D_BRANCH // LANES
CONV_STRIDE = 4
SQRT_HALF = float(np.sqrt(0.5).astype(np.float32))
VMEM_LIMIT_BYTES = 56 * 1024 * 1024

F32 = jnp.float32
BF16 = jnp.bfloat16


def _rms(x, g):
    ms = jnp.mean(x * x, axis=-1, keepdims=True)
    return x * lax.rsqrt(ms + EPS) * g


def _ln(x, g, b):
    mu = jnp.mean(x, axis=-1, keepdims=True)
    xc = x - mu
    var = jnp.mean(xc * xc, axis=-1, keepdims=True)
    return xc * lax.rsqrt(var + EPS) * g + b


def _gelu(x):
    return 0.5 * x * (1.0 + lax.erf(x * SQRT_HALF))


def _silu(x):
    return x * jax.nn.sigmoid(x)


def _dot(a, b):
    return jnp.dot(a, b, preferred_element_type=F32)


def _const_spec(shape):
    zeros = (0,) * len(shape)
    return pl.BlockSpec(shape, lambda i: zeros, pipeline_mode=pl.Buffered(1))


def _for_row_blocks(n_rows, body):
    def step(j, carry):
        body(pl.ds(pl.multiple_of(j * ROW_BLOCK, ROW_BLOCK), ROW_BLOCK))
        return carry
    lax.fori_loop(0, n_rows // ROW_BLOCK, step, 0)


def _project(x, pre_g_ref, w_in_ref, hb_ref, p_ref):
    e = D_BRANCH
    hb_ref[...] = _rms(x, pre_g_ref[...]).astype(BF16)
    for j in range(3):
        p_ref[j] = _dot(hb_ref[...], w_in_ref[:, j * e:(j + 1) * e])


def _pipelined_steps(x_ref, xn_ref, pre_g_ref, w_in_ref, hb_ref, p_even, p_odd, finish):
    i = pl.program_id(0)

    @pl.when(i == 0)
    def _():
        _project(x_ref[...], pre_g_ref, w_in_ref, hb_ref, p_even)

    def step(p_cur, p_nxt):
        _project(xn_ref[...], pre_g_ref, w_in_ref, hb_ref, p_nxt)
        finish(p_cur)

    pl.when(i % 2 == 0)(lambda: step(p_even, p_odd))
    pl.when(i % 2 == 1)(lambda: step(p_odd, p_even))


def _row_specs(n, tm, d):
    last = n // tm - 1
    return (pl.BlockSpec((tm, d), lambda i: (i, 0)),
            pl.BlockSpec((tm, d), lambda i: (jnp.minimum(i + 1, last), 0)))


def _mixer_a_kernel(x_ref, xn_ref, pre_g_ref, post_g_ref, w_in_ref, ln_g_ref, ln_b_ref, ws_ref,
                    bias_ref, w_out_ref, *refs, tm, emit_v):
    if emit_v:
        y_ref, v_ref, hb_ref, p_even, p_odd, vb_ref, o_ref = refs
    else:
        y_ref, hb_ref, p_even, p_odd, vb_ref, o_ref = refs

    def finish(p_ref):
        for r0 in range(0, tm, ELEM_ROWS):
            rows = slice(r0, r0 + ELEM_ROWS)
            v = _ln(_gelu(p_ref[1, rows, :]), ln_g_ref[...], ln_b_ref[...])
            if emit_v:
                v_ref[rows, :] = v
            vb_ref[rows, :] = v.astype(BF16)
            p_ref[0, rows, :] = _gelu(p_ref[0, rows, :]) * _silu(p_ref[2, rows, :])
        for c in range(tm // CHUNK):
            rows = slice(c * CHUNK, (c + 1) * CHUNK)
            for g in range(N_GROUPS):
                lanes = slice(g * GROUP_DIM, (g + 1) * GROUP_DIM)
                mixed = _dot(ws_ref[g], vb_ref[rows, lanes]) + bias_ref[:, lanes]
                o_ref[rows, lanes] = (p_ref[0, rows, lanes] * mixed).astype(BF16)
        y = _dot(o_ref[...], w_out_ref[...])
        y_ref[...] = x_ref[...] + _rms(y, post_g_ref[...])

    _pipelined_steps(x_ref, xn_ref, pre_g_ref, w_in_ref, hb_ref, p_even, p_odd, finish)


def _mixer_a(x2d, pre_g, post_g, w_in, ln_g, ln_b, ws, bias, w_out, *, tm, emit_v):
    n, d = x2d.shape
    e = D_BRANCH
    assert n % tm == 0 and tm % CHUNK == 0
    x_spec, xn_spec = _row_specs(n, tm, d)
    out_shape = [jax.ShapeDtypeStruct((n, d), F32)]
    out_specs = [x_spec]
    if emit_v:
        out_shape.append(jax.ShapeDtypeStruct((n, e), F32))
        out_specs.append(pl.BlockSpec((tm, e), lambda i: (i, 0)))
    return pl.pallas_call(
        functools.partial(_mixer_a_kernel, tm=tm, emit_v=emit_v),
        grid=(n // tm,),
        in_specs=[
            x_spec, xn_spec, _const_spec((1, d)), _const_spec((1, d)), _const_spec((d, 3 * e)),
            _const_spec((1, e)), _const_spec((1, e)), _const_spec((N_GROUPS, CHUNK, CHUNK)),
            _const_spec((CHUNK, e)), _const_spec((e, d)),
        ],
        out_specs=out_specs,
        out_shape=out_shape,
        scratch_shapes=[pltpu.VMEM((tm, d), BF16), pltpu.VMEM((3, tm, e), F32), pltpu.VMEM((3, tm, e), F32),
                        pltpu.VMEM((tm, e), BF16), pltpu.VMEM((tm, e), BF16)],
        compiler_params=pltpu.CompilerParams(
            dimension_semantics=("arbitrary",), vmem_limit_bytes=VMEM_LIMIT_BYTES),
        name="mixer_a_sample" if emit_v else "mixer_a_prompt",
    )(x2d, x2d, pre_g, post_g, w_in, ln_g, ln_b, ws, bias, w_out)


def _mixer_b_prompt_kernel(x_ref, xn_ref, pre_g_ref, post_g_ref, w_in_ref, cw_ref, cb_ref, ln_g_ref,
                           ln_b_ref, w_out_ref, y_ref, tail_ref, hb_ref, p_even, p_odd,
                           xp_ref, c_ref, o_ref, *, tm, tiles_per_seq):
    i = pl.program_id(0)

    @pl.when(i % tiles_per_seq == 0)
    def _():
        xp_ref[:, 0:HIST, :] = jnp.zeros((N_SLABS, HIST, LANES), F32)

    def finish(p_ref):
        for r0 in range(0, tm, ELEM_ROWS):
            rows = slice(r0, r0 + ELEM_ROWS)
            g = p_ref[0, rows, :] * jax.nn.sigmoid(p_ref[1, rows, :])
            for s in range(N_SLABS):
                xp_ref[s, HIST + r0:HIST + r0 + ELEM_ROWS, :] = g[:, s * LANES:(s + 1) * LANES]

        base = HIST - (CONV_W - 1)
        rows_per_phase = tm // CONV_STRIDE
        for s in range(N_SLABS):
            lanes = slice(s * LANES, (s + 1) * LANES)
            for p in range(CONV_STRIDE):
                acc = None
                for k in range(CONV_W):
                    taps = xp_ref[s, pl.ds(base + p + k, rows_per_phase, stride=CONV_STRIDE), :]
                    term = taps * cw_ref[k:k + 1, lanes]
                    acc = term if acc is None else acc + term
                c_ref[s, pl.ds(p, rows_per_phase, stride=CONV_STRIDE), :] = acc + cb_ref[:, lanes]

        for r0 in range(0, tm, ELEM_ROWS):
            rows = slice(r0, r0 + ELEM_ROWS)
            c = jnp.concatenate([c_ref[s, rows, :] for s in range(N_SLABS)], axis=1)
            c = _silu(_ln(c, ln_g_ref[...], ln_b_ref[...]))
            o_ref[rows, :] = (c * _silu(p_ref[2, rows, :])).astype(BF16)
        y = _dot(o_ref[...], w_out_ref[...])
        y_ref[...] = x_ref[...] + _rms(y, post_g_ref[...])
        for s in range(N_SLABS):
            hist = xp_ref[s, tm:tm + HIST, :]
            tail_ref[0, :, s * LANES:(s + 1) * LANES] = hist
            xp_ref[s, 0:HIST, :] = hist

    _pipelined_steps(x_ref, xn_ref, pre_g_ref, w_in_ref, hb_ref, p_even, p_odd, finish)


def _mixer_b_prompt(x2d, pre_g, post_g, w_in, cw, cb, ln_g, ln_b, w_out, *, tm, seq):
    n, d = x2d.shape
    e = D_BRANCH
    assert seq % tm == 0 and n % seq == 0 and tm % (8 * CONV_STRIDE) == 0
    tiles_per_seq = seq // tm
    x_spec, xn_spec = _row_specs(n, tm, d)
    return pl.pallas_call(
        functools.partial(_mixer_b_prompt_kernel, tm=tm, tiles_per_seq=tiles_per_seq),
        grid=(n // tm,),
        in_specs=[
            x_spec, xn_spec, _const_spec((1, d)), _const_spec((1, d)), _const_spec((d, 3 * e)),
            _const_spec((CONV_W, e)), _const_spec((1, e)), _const_spec((1, e)), _const_spec((1, e)),
            _const_spec((e, d)),
        ],
        out_specs=[x_spec, pl.BlockSpec((1, HIST, e), lambda i: (i // tiles_per_seq, 0, 0))],
        out_shape=[jax.ShapeDtypeStruct((n, d), F32),
                   jax.ShapeDtypeStruct((n // seq, HIST, e), F32)],
        scratch_shapes=[pltpu.VMEM((tm, d), BF16), pltpu.VMEM((3, tm, e), F32), pltpu.VMEM((3, tm, e), F32),
                        pltpu.VMEM((N_SLABS, HIST + tm, LANES), F32),
                        pltpu.VMEM((N_SLABS, tm, LANES), F32), pltpu.VMEM((tm, e), BF16)],
        compiler_params=pltpu.CompilerParams(
            dimension_semantics=("arbitrary",), vmem_limit_bytes=VMEM_LIMIT_BYTES),
        name="mixer_b_prompt",
    )(x2d, x2d, pre_g, post_g, w_in, cw, cb, ln_g, ln_b, w_out)


def _conv_tail(x, load_c, z_ref, o_ref, ln_g_ref, ln_b_ref, w_out_ref, post_g_ref):
    def act(rows):
        c = _silu(_ln(load_c(rows), ln_g_ref[...], ln_b_ref[...]))
        o_ref[rows, :] = (c * _silu(z_ref[rows, :])).astype(BF16)
    _for_row_blocks(o_ref.shape[0], act)
    y = _dot(o_ref[...], w_out_ref[...])
    return x + _rms(y, post_g_ref[...])


def _mixer_b_sample_kernel(x_ref, st_ref, pre_g_ref, post_g_ref, w_in_ref, cw_ref, wt_ref, cb_ref,
                           ln_g_ref, ln_b_ref, w_out_ref, y_ref, g_ref, hb_ref, p_ref, o_ref, *, bs, t_new):
    d, e = D_MODEL, D_BRANCH
    rows = bs * t_new
    x = x_ref[...].reshape(rows, d)
    _project(x, pre_g_ref, w_in_ref, hb_ref, p_ref)
    a_ref, z_ref = p_ref.at[0], p_ref.at[2]
    g = a_ref[...] * jax.nn.sigmoid(p_ref[1])
    g_ref[...] = g.reshape(t_new, bs, e)
    st = st_ref[...]
    for t in range(t_new):
        c_t = jnp.sum(st * wt_ref[t][None, :, :], axis=1)
        for s in range(t + 1):
            k = CONV_W - 1 - (t - s)
            c_t = c_t + g[s * bs:(s + 1) * bs, :] * cw_ref[k:k + 1, :]
        a_ref[t * bs:(t + 1) * bs, :] = c_t + cb_ref[...]
    load_c = lambda rows: a_ref[rows, :]
    y = _conv_tail(x, load_c, z_ref, o_ref, ln_g_ref, ln_b_ref, w_out_ref, post_g_ref)
    y_ref[...] = y.reshape(t_new, bs, d)


def _mixer_b_sample(x3d, state, pre_g, post_g, w_in, cw, wt, cb, ln_g, ln_b, w_out, *, bs):
    t_new, nb, d = x3d.shape
    e = D_BRANCH
    hist = state.shape[1]
    rows = bs * t_new
    assert nb % bs == 0 and rows % ROW_BLOCK == 0
    return pl.pallas_call(
        functools.partial(_mixer_b_sample_kernel, bs=bs, t_new=t_new),
        grid=(nb // bs,),
        in_specs=[
            pl.BlockSpec((t_new, bs, d), lambda i: (0, i, 0)),
            pl.BlockSpec((bs, hist, e), lambda i: (i, 0, 0)),
            _const_spec((1, d)), _const_spec((1, d)), _const_spec((d, 3 * e)),
            _const_spec((CONV_W, e)), _const_spec((t_new, hist, e)), _const_spec((1, e)),
            _const_spec((1, e)), _const_spec((1, e)), _const_spec((e, d)),
        ],
        out_specs=[pl.BlockSpec((t_new, bs, d), lambda i: (0, i, 0)),
                   pl.BlockSpec((t_new, bs, e), lambda i: (0, i, 0))],
        out_shape=[jax.ShapeDtypeStruct((t_new, nb, d), F32), jax.ShapeDtypeStruct((t_new, nb, e), F32)],
        scratch_shapes=[pltpu.VMEM((rows, d), BF16), pltpu.VMEM((3, rows, e), F32),
                        pltpu.VMEM((rows, e), BF16)],
        compiler_params=pltpu.CompilerParams(
            dimension_semantics=("arbitrary",), vmem_limit_bytes=VMEM_LIMIT_BYTES),
        name="mixer_b_sample",
    )(x3d, state, pre_g, post_g, w_in, cw, wt, cb, ln_g, ln_b, w_out)


def kernel(x_prompt, x_sample, state_conv, pre_norm_g, post_norm_g, a_w_in, a_ln_g, a_ln_b, a_w_s,
           a_b_s, a_w_out, b_w_in, b_conv_w, b_conv_b, b_ln_g, b_ln_b, b_w_out):
    batch, seq, d = x_prompt.shape
    dec_batch, dec_seq, _ = x_sample.shape
    e = D_BRANCH
    hist = CONV_W - 1
    row = lambda p: p.reshape(1, -1)

    xp = x_prompt.reshape(batch * seq, d)
    xs = x_sample.reshape(dec_batch * dec_seq, d)

    w_in = a_w_in[0].astype(BF16)
    w_out = a_w_out[0].astype(BF16)
    ws_prompt = (a_w_s[0] * jnp.tril(jnp.ones((CHUNK, CHUNK), F32))).astype(BF16)
    bias_prompt = jnp.repeat(a_b_s[0].T, GROUP_DIM, axis=1)
    reps = CHUNK // dec_seq
    w_small = a_w_s[0][:, :dec_seq, :dec_seq] * jnp.tril(jnp.ones((dec_seq, dec_seq), F32))
    ws_sample = jnp.einsum("ab,gts->gatbs", jnp.eye(reps, dtype=F32), w_small)
    ws_sample = ws_sample.reshape(N_GROUPS, CHUNK, CHUNK).astype(BF16)
    bias_sample = jnp.repeat(jnp.tile(a_b_s[0][:, :dec_seq], (1, reps)).T, GROUP_DIM, axis=1)

    a_common = (row(pre_norm_g[0]), row(post_norm_g[0]), w_in, row(a_ln_g[0]), row(a_ln_b[0]))
    (xp,) = _mixer_a(xp, *a_common, ws_prompt, bias_prompt, w_out, tm=256, emit_v=False)
    xs, v_sample = _mixer_a(xs, *a_common, ws_sample, bias_sample, w_out, tm=256, emit_v=True)

    w_in = b_w_in[0].astype(BF16)
    w_out = b_w_out[0].astype(BF16)
    cw = b_conv_w[0]
    b_common = (row(pre_norm_g[1]), row(post_norm_g[1]), w_in)
    b_tail = (row(b_conv_b[0]), row(b_ln_g[0]), row(b_ln_b[0]), w_out)
    xp, tail = _mixer_b_prompt(xp, *b_common, cw, *b_tail, tm=256, seq=seq)
    wt = jnp.stack([jnp.pad(cw[:hist - t], ((t, 0), (0, 0))) for t in range(dec_seq)])
    xs_t = xs.reshape(dec_batch, dec_seq, d).transpose(1, 0, 2)
    xs_t, g_t = _mixer_b_sample(xs_t, state_conv[0], *b_common, cw, wt, *b_tail, bs=32)

    conv_prompt_new = tail[:, HIST - hist:, :][None]
    conv_sample_new = jnp.concatenate([state_conv[0][:, dec_seq:, :], g_t.transpose(1, 0, 2)], axis=1)[None]
    return (xp.reshape(batch, seq, d), xs_t.transpose(1, 0, 2), conv_prompt_new, conv_sample_new,
            v_sample.reshape(dec_batch, dec_seq, e)[None])
```

```python
import functools

import jax
import jax.numpy as jnp
import numpy as np
from jax import lax
from jax.experimental import pallas as pl
from jax.experimental.pallas import tpu as pltpu

D_MODEL = 1024
D_BRANCH = 2048
CHUNK = 128
N_GROUPS = 8
GROUP_DIM = D_BRANCH // N_GROUPS
CONV_W = 31
EPS = 1e-6

HIST = 32
ROW_BLOCK = 32
ELEM_ROWS = 16
LANES = 128
N_SLABS = D_BRANCH // LANES
CONV_STRIDE = 4
PROJ_COLS = 512
SQRT_HALF = float(np.sqrt(0.5).astype(np.float32))
VMEM_LIMIT_BYTES = 56 * 1024 * 1024

F32 = jnp.float32
BF16 = jnp.bfloat16


def _rms(x, g):
    ms = jnp.mean(x * x, axis=-1, keepdims=True)
    return x * lax.rsqrt(ms + EPS) * g


def _ln(x, g, b):
    mu = jnp.mean(x, axis=-1, keepdims=True)
    xc = x - mu
    var = jnp.mean(xc * xc, axis=-1, keepdims=True)
    return xc * lax.rsqrt(var + EPS) * g + b


def _gelu(x):
    return 0.5 * x * (1.0 + lax.erf(x * SQRT_HALF))


def _silu(x):
    return x * jax.nn.sigmoid(x)


def _dot(a, b):
    return jnp.dot(a, b, preferred_element_type=F32)


def _const_spec(shape):
    zeros = (0,) * len(shape)
    return pl.BlockSpec(shape, lambda i: zeros, pipeline_mode=pl.Buffered(1))


def _for_row_blocks(n_rows, body):
    def step(j, carry):
        body(pl.ds(pl.multiple_of(j * ROW_BLOCK, ROW_BLOCK), ROW_BLOCK))
        return carry
    lax.fori_loop(0, n_rows // ROW_BLOCK, step, 0)


def _pack_bf16_rows(w):
    k, n = w.shape
    bits = lax.bitcast_convert_type(w.astype(BF16).astype(F32), jnp.uint32).reshape(k // 2, 2 * n)
    return (bits[:, :n] >> 16) | (bits[:, n:] & jnp.uint32(0xFFFF0000))


def _weight(w_ref, cols=slice(None)):
    return pltpu.bitcast(w_ref[:, cols], BF16)


def _proj_chunks(w_in_ref, hb_ref, p_ref):
    e = D_BRANCH

    def chunk(j, c0):
        cols = slice(c0, c0 + PROJ_COLS)
        w_cols = slice(j * e + c0, j * e + c0 + PROJ_COLS)
        p_ref[j, :, cols] = _dot(hb_ref[...], _weight(w_in_ref, w_cols))

    return [functools.partial(chunk, j, c0) for j in range(3) for c0 in range(0, e, PROJ_COLS)]


def _project(x, pre_g_ref, w_in_ref, hb_ref, p_ref):
    hb_ref[...] = _rms(x, pre_g_ref[...]).astype(BF16)
    for chunk in _proj_chunks(w_in_ref, hb_ref, p_ref):
        chunk()


def _emit_spread(major, minor):
    done = 0
    for idx, item in enumerate(major):
        item()
        due = (idx + 1) * len(minor) // len(major)
        for extra in minor[done:due]:
            extra()
        done = due


def _pipelined_steps(x_ref, xn_ref, pre_g_ref, w_in_ref, hb_ref, p_even, p_odd, finish):
    i = pl.program_id(0)

    @pl.when(i == 0)
    def _():
        _project(x_ref[...], pre_g_ref, w_in_ref, hb_ref, p_even)

    def step(p_cur, p_nxt):
        hb_ref[...] = _rms(xn_ref[...], pre_g_ref[...]).astype(BF16)
        finish(p_cur, _proj_chunks(w_in_ref, hb_ref, p_nxt))

    pl.when(i % 2 == 0)(lambda: step(p_even, p_odd))
    pl.when(i % 2 == 1)(lambda: step(p_odd, p_even))


def _row_specs(n, tm, d):
    last = n // tm - 1
    return (pl.BlockSpec((tm, d), lambda i: (i, 0)),
            pl.BlockSpec((tm, d), lambda i: (jnp.minimum(i + 1, last), 0)))


def _mixer_a_kernel(x_ref, xn_ref, pre_g_ref, post_g_ref, w_in_ref, ln_g_ref, ln_b_ref, ws_ref,
                    bias_ref, w_out_ref, *refs, tm, emit_v):
    if emit_v:
        y_ref, v_ref, hb_ref, p_even, p_odd, vb_ref, o_ref = refs
    else:
        y_ref, hb_ref, p_even, p_odd, vb_ref, o_ref = refs

    def finish(p_ref, proj):
        def gate(r0):
            rows = slice(r0, r0 + ELEM_ROWS)
            v = _ln(_gelu(p_ref[1, rows, :]), ln_g_ref[...], ln_b_ref[...])
            if emit_v:
                v_ref[rows, :] = v
            vb_ref[rows, :] = v.astype(BF16)
            p_ref[0, rows, :] = _gelu(p_ref[0, rows, :]) * _silu(p_ref[2, rows, :])

        def mix_and_out(c):
            rows = slice(c * CHUNK, (c + 1) * CHUNK)
            for g in range(N_GROUPS):
                lanes = slice(g * GROUP_DIM, (g + 1) * GROUP_DIM)
                mixed = _dot(ws_ref[g], vb_ref[rows, lanes]) + bias_ref[:, lanes]
                o_ref[rows, lanes] = (p_ref[0, rows, lanes] * mixed).astype(BF16)
            y = _dot(o_ref[rows, :], _weight(w_out_ref))
            y_ref[rows, :] = x_ref[rows, :] + _rms(y, post_g_ref[...])

        n_chunks = tm // CHUNK
        proj[0]()
        rest = proj[1:]
        for c in range(n_chunks):
            gates = [functools.partial(gate, r0) for r0 in range(c * CHUNK, (c + 1) * CHUNK, ELEM_ROWS)]
            share = rest[c * len(rest) // n_chunks:(c + 1) * len(rest) // n_chunks]
            _emit_spread(gates, share)
            mix_and_out(c)

    _pipelined_steps(x_ref, xn_ref, pre_g_ref, w_in_ref, hb_ref, p_even, p_odd, finish)


def _mixer_a(x2d, pre_g, post_g, w_in, ln_g, ln_b, ws, bias, w_out, *, tm, emit_v):
    n, d = x2d.shape
    e = D_BRANCH
    assert n % tm == 0 and tm % CHUNK == 0
    x_spec, xn_spec = _row_specs(n, tm, d)
    out_shape = [jax.ShapeDtypeStruct((n, d), F32)]
    out_specs = [x_spec]
    if emit_v:
        out_shape.append(jax.ShapeDtypeStruct((n, e), F32))
        out_specs.append(pl.BlockSpec((tm, e), lambda i: (i, 0)))
    return pl.pallas_call(
        functools.partial(_mixer_a_kernel, tm=tm, emit_v=emit_v),
        grid=(n // tm,),
        in_specs=[
            x_spec, xn_spec, _const_spec((1, d)), _const_spec((1, d)), _const_spec((d // 2, 3 * e)),
            _const_spec((1, e)), _const_spec((1, e)), _const_spec((N_GROUPS, CHUNK, CHUNK)),
            _const_spec((CHUNK, e)), _const_spec((e // 2, d)),
        ],
        out_specs=out_specs,
        out_shape=out_shape,
        scratch_shapes=[pltpu.VMEM((tm, d), BF16), pltpu.VMEM((3, tm, e), F32), pltpu.VMEM((3, tm, e), F32),
                        pltpu.VMEM((tm, e), BF16), pltpu.VMEM((tm, e), BF16)],
        compiler_params=pltpu.CompilerParams(
            dimension_semantics=("arbitrary",), vmem_limit_bytes=VMEM_LIMIT_BYTES),
        name="mixer_a_sample" if emit_v else "mixer_a_prompt",
    )(x2d, x2d, pre_g, post_g, w_in, ln_g, ln_b, ws, bias, w_out)


def _mixer_b_prompt_kernel(x_ref, xn_ref, pre_g_ref, post_g_ref, w_in_ref, cw_ref, cb_ref, ln_g_ref,
                           ln_b_ref, w_out_ref, y_ref, tail_ref, hb_ref, p_even, p_odd,
                           xp_ref, c_ref, o_ref, *, tm, tiles_per_seq):
    i = pl.program_id(0)

    @pl.when(i % tiles_per_seq == 0)
    def _():
        xp_ref[:, 0:HIST, :] = jnp.zeros((N_SLABS, HIST, LANES), F32)

    def finish(p_ref, proj):
        def glu(r0):
            rows = slice(r0, r0 + ELEM_ROWS)
            g = p_ref[0, rows, :] * jax.nn.sigmoid(p_ref[1, rows, :])
            for s in range(N_SLABS):
                xp_ref[s, HIST + r0:HIST + r0 + ELEM_ROWS, :] = g[:, s * LANES:(s + 1) * LANES]

        base = HIST - (CONV_W - 1)
        rows_per_phase = tm // CONV_STRIDE

        def conv(s, p):
            lanes = slice(s * LANES, (s + 1) * LANES)
            acc = None
            for k in range(CONV_W):
                taps = xp_ref[s, pl.ds(base + p + k, rows_per_phase, stride=CONV_STRIDE), :]
                term = taps * cw_ref[k:k + 1, lanes]
                acc = term if acc is None else acc + term
            c_ref[s, pl.ds(p, rows_per_phase, stride=CONV_STRIDE), :] = acc + cb_ref[:, lanes]

        def act(r0):
            rows = slice(r0, r0 + ELEM_ROWS)
            c = jnp.concatenate([c_ref[s, rows, :] for s in range(N_SLABS)], axis=1)
            c = _silu(_ln(c, ln_g_ref[...], ln_b_ref[...]))
            o_ref[rows, :] = (c * _silu(p_ref[2, rows, :])).astype(BF16)

        def out(rows):
            y = _dot(o_ref[rows, :], _weight(w_out_ref))
            y_ref[rows, :] = x_ref[rows, :] + _rms(y, post_g_ref[...])

        proj[0]()
        vector_items = [functools.partial(glu, r0) for r0 in range(0, tm, ELEM_ROWS)]
        vector_items += [functools.partial(conv, s, p) for s in range(N_SLABS) for p in range(CONV_STRIDE)]
        _emit_spread(vector_items, proj[1:])
        half = tm // 2
        for h in range(2):
            for r0 in range(h * half, (h + 1) * half, ELEM_ROWS):
                act(r0)
            out(slice(h * half, (h + 1) * half))
        for s in range(N_SLABS):
            hist = xp_ref[s, tm:tm + HIST, :]
            tail_ref[0, :, s * LANES:(s + 1) * LANES] = hist
            xp_ref[s, 0:HIST, :] = hist

    _pipelined_steps(x_ref, xn_ref, pre_g_ref, w_in_ref, hb_ref, p_even, p_odd, finish)


def _mixer_b_prompt(x2d, pre_g, post_g, w_in, cw, cb, ln_g, ln_b, w_out, *, tm, seq):
    n, d = x2d.shape
    e = D_BRANCH
    assert seq % tm == 0 and n % seq == 0 and tm % (8 * CONV_STRIDE) == 0
    tiles_per_seq = seq // tm
    x_spec, xn_spec = _row_specs(n, tm, d)
    return pl.pallas_call(
        functools.partial(_mixer_b_prompt_kernel, tm=tm, tiles_per_seq=tiles_per_seq),
        grid=(n // tm,),
        in_specs=[
            x_spec, xn_spec, _const_spec((1, d)), _const_spec((1, d)), _const_spec((d // 2, 3 * e)),
            _const_spec((CONV_W, e)), _const_spec((1, e)), _const_spec((1, e)), _const_spec((1, e)),
            _const_spec((e // 2, d)),
        ],
        out_specs=[x_spec, pl.BlockSpec((1, HIST, e), lambda i: (i // tiles_per_seq, 0, 0))],
        out_shape=[jax.ShapeDtypeStruct((n, d), F32),
                   jax.ShapeDtypeStruct((n // seq, HIST, e), F32)],
        scratch_shapes=[pltpu.VMEM((tm, d), BF16), pltpu.VMEM((3, tm, e), F32), pltpu.VMEM((3, tm, e), F32),
                        pltpu.VMEM((N_SLABS, HIST + tm, LANES), F32),
                        pltpu.VMEM((N_SLABS, tm, LANES), F32), pltpu.VMEM((tm, e), BF16)],
        compiler_params=pltpu.CompilerParams(
            dimension_semantics=("arbitrary",), vmem_limit_bytes=VMEM_LIMIT_BYTES),
        name="mixer_b_prompt",
    )(x2d, x2d, pre_g, post_g, w_in, cw, cb, ln_g, ln_b, w_out)


def _conv_tail(x, load_c, z_ref, o_ref, ln_g_ref, ln_b_ref, w_out_ref, post_g_ref):
    def act(rows):
        c = _silu(_ln(load_c(rows), ln_g_ref[...], ln_b_ref[...]))
        o_ref[rows, :] = (c * _silu(z_ref[rows, :])).astype(BF16)
    _for_row_blocks(o_ref.shape[0], act)
    y = _dot(o_ref[...], _weight(w_out_ref))
    return x + _rms(y, post_g_ref[...])


def _mixer_b_sample_kernel(x_ref, st_ref, pre_g_ref, post_g_ref, w_in_ref, cw_ref, wt_ref, cb_ref,
                           ln_g_ref, ln_b_ref, w_out_ref, y_ref, g_ref, hb_ref, p_ref, o_ref, *, bs, t_new):
    d, e = D_MODEL, D_BRANCH
    rows = bs * t_new
    x = x_ref[...].reshape(rows, d)
    _project(x, pre_g_ref, w_in_ref, hb_ref, p_ref)
    a_ref, z_ref = p_ref.at[0], p_ref.at[2]
    g = a_ref[...] * jax.nn.sigmoid(p_ref[1])
    g_ref[...] = g.reshape(t_new, bs, e)
    st = st_ref[...]
    for t in range(t_new):
        c_t = jnp.sum(st * wt_ref[t][None, :, :], axis=1)
        for s in range(t + 1):
            k = CONV_W - 1 - (t - s)
            c_t = c_t + g[s * bs:(s + 1) * bs, :] * cw_ref[k:k + 1, :]
        a_ref[t * bs:(t + 1) * bs, :] = c_t + cb_ref[...]
    load_c = lambda rows: a_ref[rows, :]
    y = _conv_tail(x, load_c, z_ref, o_ref, ln_g_ref, ln_b_ref, w_out_ref, post_g_ref)
    y_ref[...] = y.reshape(t_new, bs, d)


def _mixer_b_sample(x3d, state, pre_g, post_g, w_in, cw, wt, cb, ln_g, ln_b, w_out, *, bs):
    t_new, nb, d = x3d.shape
    e = D_BRANCH
    n_layers, _, hist, _ = state.shape
    assert n_layers == 1
    rows = bs * t_new
    assert nb % bs == 0 and rows % ROW_BLOCK == 0
    return pl.pallas_call(
        functools.partial(_mixer_b_sample_kernel, bs=bs, t_new=t_new),
        grid=(nb // bs,),
        in_specs=[
            pl.BlockSpec((t_new, bs, d), lambda i: (0, i, 0)),
            pl.BlockSpec((None, bs, hist, e), lambda i: (0, i, 0, 0)),
            _const_spec((1, d)), _const_spec((1, d)), _const_spec((d // 2, 3 * e)),
            _const_spec((CONV_W, e)), _const_spec((t_new, hist, e)), _const_spec((1, e)),
            _const_spec((1, e)), _const_spec((1, e)), _const_spec((e // 2, d)),
        ],
        out_specs=[pl.BlockSpec((t_new, bs, d), lambda i: (0, i, 0)),
                   pl.BlockSpec((t_new, bs, e), lambda i: (0, i, 0))],
        out_shape=[jax.ShapeDtypeStruct((t_new, nb, d), F32), jax.ShapeDtypeStruct((t_new, nb, e), F32)],
        scratch_shapes=[pltpu.VMEM((rows, d), BF16), pltpu.VMEM((3, rows, e), F32),
                        pltpu.VMEM((rows, e), BF16)],
        compiler_params=pltpu.CompilerParams(
            dimension_semantics=("arbitrary",), vmem_limit_bytes=VMEM_LIMIT_BYTES),
        name="mixer_b_sample",
    )(x3d, state, pre_g, post_g, w_in, cw, wt, cb, ln_g, ln_b, w_out)


def kernel(x_prompt, x_sample, state_conv, pre_norm_g, post_norm_g, a_w_in, a_ln_g, a_ln_b, a_w_s,
           a_b_s, a_w_out, b_w_in, b_conv_w, b_conv_b, b_ln_g, b_ln_b, b_w_out):
    batch, seq, d = x_prompt.shape
    dec_batch, dec_seq, _ = x_sample.shape
    e = D_BRANCH
    hist = CONV_W - 1
    row = lambda p: p.reshape(1, -1)

    xp = x_prompt.reshape(batch * seq, d)
    xs = x_sample.reshape(dec_batch * dec_seq, d)

    w_in = _pack_bf16_rows(a_w_in[0])
    w_out = _pack_bf16_rows(a_w_out[0])
    ws_prompt = (a_w_s[0] * jnp.tril(jnp.ones((CHUNK, CHUNK), F32))).astype(BF16)
    bias_prompt = jnp.repeat(a_b_s[0].T, GROUP_DIM, axis=1)
    reps = CHUNK // dec_seq
    w_small = a_w_s[0][:, :dec_seq, :dec_seq] * jnp.tril(jnp.ones((dec_seq, dec_seq), F32))
    ws_sample = jnp.einsum("ab,gts->gatbs", jnp.eye(reps, dtype=F32), w_small)
    ws_sample = ws_sample.reshape(N_GROUPS, CHUNK, CHUNK).astype(BF16)
    bias_sample = jnp.repeat(jnp.tile(a_b_s[0][:, :dec_seq], (1, reps)).T, GROUP_DIM, axis=1)

    a_common = (row(pre_norm_g[0]), row(post_norm_g[0]), w_in, row(a_ln_g[0]), row(a_ln_b[0]))
    (xp,) = _mixer_a(xp, *a_common, ws_prompt, bias_prompt, w_out, tm=256, emit_v=False)
    xs, v_sample = _mixer_a(xs, *a_common, ws_sample, bias_sample, w_out, tm=256, emit_v=True)

    w_in = _pack_bf16_rows(b_w_in[0])
    w_out = _pack_bf16_rows(b_w_out[0])
    cw = b_conv_w[0]
    b_common = (row(pre_norm_g[1]), row(post_norm_g[1]), w_in)
    b_tail = (row(b_conv_b[0]), row(b_ln_g[0]), row(b_ln_b[0]), w_out)
    xp, tail = _mixer_b_prompt(xp, *b_common, cw, *b_tail, tm=256, seq=seq)
    wt = jnp.stack([jnp.pad(cw[:hist - t], ((t, 0), (0, 0))) for t in range(dec_seq)])
    xs_t = xs.reshape(dec_batch, dec_seq, d).transpose(1, 0, 2)
    xs_t, g_t = _mixer_b_sample(xs_t, state_conv, *b_common, cw, wt, *b_tail, bs=32)

    conv_prompt_new = tail[:, HIST - hist:, :][None]
    conv_sample_new = jnp.concatenate([state_conv[:, :, dec_seq:, :], g_t.transpose(1, 0, 2)[None]], axis=2)
    return (xp.reshape(batch, seq, d), xs_t.transpose(1, 0, 2), conv_prompt_new, conv_sample_new,
            v_sample.reshape(dec_batch, dec_seq, e)[None])
```
